```python
import math
import jax, jax.numpy as jnp
from jax import lax
import numpy as np

D_MODEL = 1024
BATCH = 4
SEQ = 4096
DEPTH = 4
DEC_BATCH = 128
DEC_SEQ = 4
PAST_LEN = 8192
PAGE_SIZE = 128

N_A_LAYERS = DEPTH // 2
N_B_LAYERS = DEPTH - N_A_LAYERS
HEAD_DIM = 64
A_HEADS = D_MODEL // HEAD_DIM
A_GROUPS = ((128, 1), (512, 4), (2048, 16))
N_A_GROUPS = len(A_GROUPS)
SUB_WINDOW = 128
B_HEADS = D_MODEL // HEAD_DIM
B_KV_HEADS = B_HEADS // 8
B_WINDOW = 128
BLOCK = 128
ROPE_THETA = 10000.0
NORM_EPS = 1e-6
PEER_HEADS = 8
PEER_N_KEYS = 128
PEER_N_EXPERTS = PEER_N_KEYS ** 2
PEER_TOPK = 16
PEER_KEY_DIM = 128
PEER_CHUNK = 256
NEG_INF = -1e30

kernel_name = 'yoco_dilated_swa_sink_peer_step'


def rmsnorm(x, g):
    xf = x.astype(jnp.float32)
    y = xf * lax.rsqrt(jnp.mean(xf * xf, axis=-1, keepdims=True) + NORM_EPS)
    return (y * g.astype(jnp.float32)).astype(x.dtype)


def rope(x, positions):
    half = HEAD_DIM // 2
    inv_freq = ROPE_THETA ** (-jnp.arange(half, dtype=jnp.float32) / half)
    ang = positions.astype(jnp.float32)[:, None] * inv_freq[None, :]
    shape = (ang.shape[0],) + (1,) * (x.ndim - 3) + (half,)
    cos = jnp.cos(ang).reshape(shape)
    sin = jnp.sin(ang).reshape(shape)
    xf = x.astype(jnp.float32)
    x1, x2 = xf[..., :half], xf[..., half:]
    return jnp.concatenate([x1 * cos - x2 * sin, x2 * cos + x1 * sin], axis=-1).astype(x.dtype)


def masked_softmax(s, mask, sink=None):
    s = jnp.where(mask, s, NEG_INF)
    m = jnp.max(s, axis=-1, keepdims=True)
    if sink is not None:
        m = jnp.maximum(m, sink)
    p = jnp.exp(s - m)
    denom = jnp.sum(p, axis=-1, keepdims=True)
    if sink is not None:
        denom = denom + jnp.exp(sink - m)
    return p / denom, (m + jnp.log(denom))[..., 0]


def banded_window_attn(q, k, v, sink=None):
    n, l, hq, hd = q.shape
    hkv = k.shape[2]
    g = hq // hkv
    nb = -(-l // BLOCK)
    lp = nb * BLOCK
    pad = ((0, 0), (0, lp - l), (0, 0), (0, 0))
    qb = jnp.pad(q, pad).reshape(n, nb, BLOCK, hkv, g, hd)
    kb = jnp.pad(k, pad).reshape(n, nb, BLOCK, hkv, hd)
    vb = jnp.pad(v, pad).reshape(n, nb, BLOCK, hkv, hd)
    prev = ((0, 0), (1, 0), (0, 0), (0, 0), (0, 0))
    kk = jnp.concatenate([jnp.pad(kb, prev)[:, :-1], kb], axis=2)
    vv = jnp.concatenate([jnp.pad(vb, prev)[:, :-1], vb], axis=2)
    s = jnp.einsum('nbqhgd,nbkhd->nbhgqk', qb, kk,
                   preferred_element_type=jnp.float32) / math.sqrt(hd)
    qi = jnp.arange(BLOCK)[:, None] + BLOCK
    ki = jnp.arange(2 * BLOCK)[None, :]
    dist = qi - ki
    kpos = jnp.arange(nb)[:, None, None] * BLOCK - BLOCK + ki[None]
    mask = (dist >= 0) & (dist <= SUB_WINDOW) & (kpos >= 0)
    mask = mask[None, :, None, None]
    sk = None if sink is None else sink.astype(jnp.float32)[None, None, :, :, None, None]
    p, lse = masked_softmax(s, mask, sk)
    out = jnp.einsum('nbhgqk,nbkhd->nbqhgd', p.astype(v.dtype), vv,
                     preferred_element_type=jnp.float32).astype(q.dtype)
    out = out.reshape(n, lp, hq, hd)[:, :l]
    lse = lse.transpose(0, 1, 4, 2, 3).reshape(n, lp, hq)[:, :l]
    return out, lse


def gathered_window_attn(q, k_all, v_all, dilation, sink=None):
    n, t, hq, hd = q.shape
    lc, hkv = k_all.shape[1], k_all.shape[2]
    g = hq // hkv
    row = (lc - t) + jnp.arange(t)[:, None] - dilation * jnp.arange(SUB_WINDOW + 1)[None, :]
    valid = row >= 0
    row = jnp.maximum(row, 0)
    kg = k_all[:, row]
    vg = v_all[:, row]
    qg = q.reshape(n, t, hkv, g, hd)
    s = jnp.einsum('nthgd,ntkhd->nthgk', qg, kg,
                   preferred_element_type=jnp.float32) / math.sqrt(hd)
    mask = valid[None, :, None, None, :]
    sk = None if sink is None else sink.astype(jnp.float32)[None, None, :, :, None]
    p, lse = masked_softmax(s, mask, sk)
    out = jnp.einsum('nthgk,ntkhd->nthgd', p.astype(v_all.dtype), vg,
                     preferred_element_type=jnp.float32).astype(q.dtype)
    return out.reshape(n, t, hq, hd), lse.reshape(n, t, hq)


def to_sub(x, d):
    n, s = x.shape[:2]
    rest = x.shape[2:]
    return x.reshape((n, s // d, d) + rest).swapaxes(1, 2).reshape((n * d, s // d) + rest)


def from_sub(x, n, d):
    l = x.shape[1]
    rest = x.shape[2:]
    return x.reshape((n, d, l) + rest).swapaxes(1, 2).reshape((n, l * d) + rest)


def peer_ffn(h, w_pq, peer_keys, peer_u, peer_v):
    t = h.shape[0]
    tp = -(-t // PEER_CHUNK) * PEER_CHUNK
    hp = jnp.pad(h, ((0, tp - t), (0, 0))).reshape(tp // PEER_CHUNK, PEER_CHUNK, D_MODEL)

    def chunk(hc):
        q = (hc @ w_pq).reshape(PEER_CHUNK, PEER_HEADS, 2, PEER_KEY_DIM)
        s = jnp.einsum('chpd,hpkd->chpk', q, peer_keys, preferred_element_type=jnp.float32)
        s1, i1 = lax.top_k(s[:, :, 0], PEER_TOPK)
        s2, i2 = lax.top_k(s[:, :, 1], PEER_TOPK)
        cand = (s1[..., :, None] + s2[..., None, :]).reshape(PEER_CHUNK, PEER_HEADS, -1)
        cid = (i1[..., :, None] * PEER_N_KEYS + i2[..., None, :]).reshape(PEER_CHUNK, PEER_HEADS, -1)
        top_s, pos = lax.top_k(cand, PEER_TOPK)
        eid = jnp.take_along_axis(cid, pos, axis=-1)
        gate = jax.nn.softmax(top_s, axis=-1)
        ue = peer_u[eid]
        ve = peer_v[eid]
        act = jax.nn.gelu(jnp.einsum('cd,chkd->chk', hc, ue, preferred_element_type=jnp.float32),
                          approximate=False)
        return jnp.einsum('chk,chkd->cd', (gate * act).astype(hc.dtype), ve,
                          preferred_element_type=jnp.float32).astype(hc.dtype)

    return lax.map(chunk, hp).reshape(tp, D_MODEL)[:t]


def _trunk(x, positions, a_bufs, b_buf, norm_attn, norm_ffn, w_qkv_a, qnorm_a, knorm_a, w_o_a,
           norm_kv_b, w_kv_b, knorm_b, w_q_b, qnorm_b, sinks_b, w_o_b,
           w_pq, peer_keys, peer_u, peer_v):
    nbat, t, _ = x.shape
    prompt = a_bufs is None
    a_new = [[] for _ in A_GROUPS]
    b_new = None
    kb = vb = kb_all = vb_all = None
    for layer in range(DEPTH):
        h = rmsnorm(x, norm_attn[layer])
        if layer < N_A_LAYERS:
            qkv = (h @ w_qkv_a[layer]).reshape(nbat, t, 3, N_A_GROUPS, A_HEADS, HEAD_DIM)
            q = rope(rmsnorm(qkv[:, :, 0], qnorm_a[layer]), positions)
            k = rope(rmsnorm(qkv[:, :, 1], knorm_a[layer]), positions)
            v = qkv[:, :, 2]
            outs, lses = [], []
            for gi, (window, dil) in enumerate(A_GROUPS):
                qg, kg, vg = q[:, :, gi], k[:, :, gi], v[:, :, gi]
                if prompt:
                    o, lse = banded_window_attn(to_sub(qg, dil), to_sub(kg, dil), to_sub(vg, dil))
                    o, lse = from_sub(o, nbat, dil), from_sub(lse, nbat, dil)
                    keep = min(window, t)
                    a_new[gi].append(jnp.stack([kg[:, t - keep:], vg[:, t - keep:]], axis=2))
                else:
                    buf = a_bufs[gi][layer]
                    k_all = jnp.concatenate([buf[:, :, 0], kg], axis=1)
                    v_all = jnp.concatenate([buf[:, :, 1], vg], axis=1)
                    o, lse = gathered_window_attn(qg, k_all, v_all, dil)
                    a_new[gi].append(jnp.stack([kg, vg], axis=2))
                outs.append(o)
                lses.append(lse)
            alpha = jax.nn.softmax(jnp.stack(lses, axis=0), axis=0)[..., None]
            mix = jnp.sum(alpha * jnp.stack(outs, axis=0).astype(jnp.float32), axis=0).astype(x.dtype)
            x = x + mix.reshape(nbat, t, A_HEADS * HEAD_DIM) @ w_o_a[layer]
        else:
            j = layer - N_A_LAYERS
            if j == 0:
                kv = (rmsnorm(x, norm_kv_b) @ w_kv_b).reshape(nbat, t, 2, B_KV_HEADS, HEAD_DIM)
                kb = rope(rmsnorm(kv[:, :, 0], knorm_b), positions)
                vb = kv[:, :, 1]
                if prompt:
                    keep = min(B_WINDOW, t)
                    b_new = jnp.stack([kb[:, t - keep:], vb[:, t - keep:]], axis=2)
                else:
                    b_new = jnp.stack([kb, vb], axis=2)
                    kb_all = jnp.concatenate([b_buf[:, :, 0], kb], axis=1)
                    vb_all = jnp.concatenate([b_buf[:, :, 1], vb], axis=1)
            q = (h @ w_q_b[j]).reshape(nbat, t, B_HEADS, HEAD_DIM)
            q = rope(rmsnorm(q, qnorm_b[j]), positions)
            sink = sinks_b[j].reshape(B_KV_HEADS, B_HEADS // B_KV_HEADS)
            if prompt:
                o, _ = banded_window_attn(q, kb, vb, sink)
            else:
                o, _ = gathered_window_attn(q, kb_all, vb_all, 1, sink)
            x = x + o.reshape(nbat, t, B_HEADS * HEAD_DIM) @ w_o_b[j]
        h = rmsnorm(x, norm_ffn[layer])
        x = x + peer_ffn(h.reshape(nbat * t, D_MODEL), w_pq[layer], peer_keys[layer],
                         peer_u[layer], peer_v[layer]).reshape(nbat, t, D_MODEL)
    a_new = [jnp.stack(rows, axis=0) for rows in a_new]
    return x, a_new, b_new


def setup_inputs(seed: int = 0) -> dict:
    key = jax.random.key(seed)
    ks = iter(jax.random.split(key, 32))

    def nrm(shape, scale):
        return scale * jax.random.normal(next(ks), shape, jnp.float32)

    def gain(shape):
        return 1.0 + nrm(shape, 0.02)

    a_cols = 3 * N_A_GROUPS * A_HEADS * HEAD_DIM
    return {
        'x_prompt': nrm((BATCH, SEQ, D_MODEL), 1.0),
        'x_sample': nrm((DEC_BATCH, DEC_SEQ, D_MODEL), 1.0),
        'cache_a_w128': nrm((N_A_LAYERS, DEC_BATCH, min(A_GROUPS[0][0], PAST_LEN), 2, A_HEADS, HEAD_DIM), 1.0),
        'cache_a_w512': nrm((N_A_LAYERS, DEC_BATCH, min(A_GROUPS[1][0], PAST_LEN), 2, A_HEADS, HEAD_DIM), 1.0),
        'cache_a_w2048': nrm((N_A_LAYERS, DEC_BATCH, min(A_GROUPS[2][0], PAST_LEN), 2, A_HEADS, HEAD_DIM), 1.0),
        'cache_b_kv': nrm((DEC_BATCH, min(B_WINDOW, PAST_LEN), 2, B_KV_HEADS, HEAD_DIM), 1.0),
        'norm_attn': gain((DEPTH, D_MODEL)),
        'norm_ffn': gain((DEPTH, D_MODEL)),
        'w_qkv_a': nrm((N_A_LAYERS, D_MODEL, a_cols), D_MODEL ** -0.5),
        'qnorm_a': gain((N_A_LAYERS, HEAD_DIM)),
        'knorm_a': gain((N_A_LAYERS, HEAD_DIM)),
        'w_o_a': nrm((N_A_LAYERS, A_HEADS * HEAD_DIM, D_MODEL), (A_HEADS * HEAD_DIM) ** -0.5),
        'norm_kv_b': gain((D_MODEL,)),
        'w_kv_b': nrm((D_MODEL, 2 * B_KV_HEADS * HEAD_DIM), D_MODEL ** -0.5),
        'knorm_b': gain((HEAD_DIM,)),
        'w_q_b': nrm((N_B_LAYERS, D_MODEL, B_HEADS * HEAD_DIM), D_MODEL ** -0.5),
        'qnorm_b': gain((N_B_LAYERS, HEAD_DIM)),
        'sinks_b': nrm((N_B_LAYERS, B_HEADS), 0.5),
        'w_o_b': nrm((N_B_LAYERS, B_HEADS * HEAD_DIM, D_MODEL), (B_HEADS * HEAD_DIM) ** -0.5),
        'w_pq': nrm((DEPTH, D_MODEL, PEER_HEADS * 2 * PEER_KEY_DIM), D_MODEL ** -0.5),
        'peer_keys': nrm((DEPTH, PEER_HEADS, 2, PEER_N_KEYS, PEER_KEY_DIM), PEER_KEY_DIM ** -0.5),
        'peer_u': nrm((DEPTH, PEER_N_EXPERTS, D_MODEL), D_MODEL ** -0.5),
        'peer_v': nrm((DEPTH, PEER_N_EXPERTS, D_MODEL), 0.2),
    }


def reference(x_prompt, x_sample, cache_a_w128, cache_a_w512, cache_a_w2048, cache_b_kv,
              norm_attn, norm_ffn, w_qkv_a, qnorm_a, knorm_a, w_o_a,
              norm_kv_b, w_kv_b, knorm_b, w_q_b, qnorm_b, sinks_b, w_o_b,
              w_pq, peer_keys, peer_u, peer_v):
    pos_prompt = jnp.arange(x_prompt.shape[1], dtype=jnp.int32)
    pos_sample = PAST_LEN + jnp.arange(x_sample.shape[1], dtype=jnp.int32)
    y_prompt, a_p, b_p = _trunk(x_prompt, pos_prompt, None, None,
                                norm_attn, norm_ffn, w_qkv_a, qnorm_a, knorm_a, w_o_a,
                                norm_kv_b, w_kv_b, knorm_b, w_q_b, qnorm_b, sinks_b, w_o_b,
                                w_pq, peer_keys, peer_u, peer_v)
    y_sample, a_s, b_s = _trunk(x_sample, pos_sample, (cache_a_w128, cache_a_w512, cache_a_w2048), cache_b_kv,
                                norm_attn, norm_ffn, w_qkv_a, qnorm_a, knorm_a, w_o_a,
                                norm_kv_b, w_kv_b, knorm_b, w_q_b, qnorm_b, sinks_b, w_o_b,
                                w_pq, peer_keys, peer_u, peer_v)
    return (y_prompt, y_sample, a_p[0], a_p[1], a_p[2], b_p, a_s[0], a_s[1], a_s[2], b_s)
```

```python
import functools
import math

import jax
import jax.numpy as jnp
from jax import lax
from jax.experimental import pallas as pl
from jax.experimental.pallas import tpu as pltpu

F32 = jnp.float32
BF16 = jnp.bfloat16

LANES = 128
SUBLANES = 8
BF16_ROWS = 16
VMEM_LIMIT = 48 * 1024 * 1024

HEAD_DIM = 64
HALF_DIM = HEAD_DIM // 2
N_HEADS = 16
D_MODEL = N_HEADS * HEAD_DIM
A_DILATIONS = (1, 4, 16)
A_WINDOWS = (128, 512, 2048)
BLOCK = 128
ROPE_THETA = 10000.0
NORM_EPS = 1e-6
NEG_INF = -1e30
PEER_HEADS = 8
PEER_KEYS = 128
PEER_TOPK = 16
PEER_KEY_DIM = 128
N_EXPERTS = PEER_KEYS * PEER_KEYS
PAST_LEN = 8192
SORT_PAD = -3.0e38


def _cparams(*sem):
    return pltpu.CompilerParams(dimension_semantics=sem, vmem_limit_bytes=VMEM_LIMIT)


def _rmsnorm_bf16(x, g):
    ms = jnp.mean(x * x, axis=-1, keepdims=True)
    return (x * lax.rsqrt(ms + NORM_EPS) * g).astype(BF16)


def _split_bf16(x):
    hi = x.astype(BF16)
    lo = (x - hi.astype(F32)).astype(BF16)
    return hi, lo


def _dot(a, b):
    return jnp.dot(a, b, preferred_element_type=F32)


def _dot_nt(a, b):
    return lax.dot_general(a, b, (((1,), (1,)), ((), ())), preferred_element_type=F32)


def _dot_split(x, w01):
    hi, lo = _split_bf16(x)
    return _dot(hi, w01) + _dot(lo, w01)


def _headnorm_rope(y, gain, cosf, sins, bd):
    tm, tn = y.shape
    lane = lax.broadcasted_iota(jnp.int32, (tm, LANES), 1)
    first_half = (lane & HALF_DIM) == 0
    outs = []
    for c in range(tn // LANES):
        t = y[:, c * LANES:(c + 1) * LANES]
        ms = _dot_split(t * t, bd)
        tn_ = t * lax.rsqrt(ms + NORM_EPS) * gain
        partner = jnp.where(first_half, pltpu.roll(tn_, LANES - HALF_DIM, 1), pltpu.roll(tn_, HALF_DIM, 1))
        outs.append(tn_ * cosf + partner * sins)
    return outs


def _norm_matmul_body(x_ref, g_ref, w_ref, gq_ref, gk_ref, cos_ref, sin_ref, bd_ref, o_ref, h_ref,
                      *, n_q_tiles, n_qk_tiles):
    j = pl.program_id(1)

    @pl.when(j == 0)
    def _():
        h_ref[...] = _rmsnorm_bf16(x_ref[...], g_ref[...])

    y = _dot(h_ref[...], w_ref[...])
    tn = y.shape[1]

    if n_qk_tiles > 0:
        @pl.when(j < n_qk_tiles)
        def _():
            gain = jnp.where(j < n_q_tiles, gq_ref[...], gk_ref[...])
            outs = _headnorm_rope(y, gain, cos_ref[...], sin_ref[...], bd_ref[...])
            for c, o in enumerate(outs):
                o_ref[:, c * LANES:(c + 1) * LANES] = o

        @pl.when(j >= n_qk_tiles)
        def _():
            o_ref[...] = y
    else:
        o_ref[...] = y


def _norm_matmul(x, g, w, tm, tn, rope=None, n_q_tiles=0, n_qk_tiles=0):
    t, d = x.shape
    n = w.shape[1]
    gq, gk, cosf, sins, bd = rope
    pos_blocks = cosf.shape[0] // tm
    body = functools.partial(_norm_matmul_body, n_q_tiles=n_q_tiles, n_qk_tiles=n_qk_tiles)
    return pl.pallas_call(
        body,
        grid=(t // tm, n // tn),
        in_specs=[
            pl.BlockSpec((tm, d), lambda i, j: (i, 0)),
            pl.BlockSpec((1, d), lambda i, j: (0, 0)),
            pl.BlockSpec((d, tn), lambda i, j: (0, j)),
            pl.BlockSpec((1, LANES), lambda i, j: (0, 0)),
            pl.BlockSpec((1, LANES), lambda i, j: (0, 0)),
            pl.BlockSpec((tm, LANES), lambda i, j: (i % pos_blocks, 0)),
            pl.BlockSpec((tm, LANES), lambda i, j: (i % pos_blocks, 0)),
            pl.BlockSpec((LANES, LANES), lambda i, j: (0, 0)),
        ],
        out_specs=pl.BlockSpec((tm, tn), lambda i, j: (i, j)),
        out_shape=jax.ShapeDtypeStruct((t, n), F32),
        scratch_shapes=[pltpu.VMEM((tm, d), BF16)],
        compiler_params=_cparams("parallel", "arbitrary"),
    )(x, g.reshape(1, d), w, gq, gk, cosf, sins, bd)


def _proj_body(x_ref, a_ref, w_ref, o_ref):
    o_ref[...] = x_ref[...] + _dot(a_ref[...].astype(BF16), w_ref[...])


def _proj_residual(x, a, w, tm):
    t, d = x.shape
    return pl.pallas_call(
        _proj_body,
        grid=(t // tm,),
        in_specs=[pl.BlockSpec((tm, d), lambda i: (i, 0)),
                  pl.BlockSpec((tm, a.shape[1]), lambda i: (i, 0)),
                  pl.BlockSpec(w.shape, lambda i: (0, 0))],
        out_specs=pl.BlockSpec((tm, d), lambda i: (i, 0)),
        out_shape=jax.ShapeDtypeStruct((t, d), F32),
        compiler_params=_cparams("parallel"),
    )(x, a, w)


def _merge_proj_body(x_ref, o1_ref, o2_ref, o3_ref, l1_ref, l2_ref, l3_ref, ex_ref, w_ref, o_ref):
    ls = [l1_ref[...], l2_ref[...], l3_ref[...]]
    m = jnp.maximum(jnp.maximum(ls[0], ls[1]), ls[2])
    es = [jnp.exp(l - m) for l in ls]
    z = es[0] + es[1] + es[2]
    mix = None
    for e, o_ref_g in zip(es, (o1_ref, o2_ref, o3_ref)):
        alpha = _dot_split(e / z, ex_ref[...])
        term = alpha * o_ref_g[...]
        mix = term if mix is None else mix + term
    o_ref[...] = x_ref[...] + _dot(mix.astype(BF16), w_ref[...])


def _merge_proj_residual(x, outs, lses, head_expand, w, tm):
    t, d = x.shape
    row = lambda i: (i, 0)
    return pl.pallas_call(
        _merge_proj_body,
        grid=(t // tm,),
        in_specs=[pl.BlockSpec((tm, d), row)] + [pl.BlockSpec((tm, d), row)] * 3
                 + [pl.BlockSpec((tm, LANES), row)] * 3
                 + [pl.BlockSpec(head_expand.shape, lambda i: (0, 0)), pl.BlockSpec(w.shape, lambda i: (0, 0))],
        out_specs=pl.BlockSpec((tm, d), row),
        out_shape=jax.ShapeDtypeStruct((t, d), F32),
        compiler_params=_cparams("parallel"),
    )(x, *outs, *lses, head_expand, w)


def _band_attn_body(*refs, shared_kv):
    if shared_kv:
        sink_ref, q_ref, kp_ref, kc_ref, vp_ref, vc_ref, o_ref = refs
    else:
        q_ref, kp_ref, kc_ref, vp_ref, vc_ref, o_ref, lse_ref = refs
    first_block = pl.program_id(2) == 0
    low = jnp.where(first_block, BLOCK, 0)
    r = lax.broadcasted_iota(jnp.int32, (BLOCK, 2 * BLOCK), 0)
    c = lax.broadcasted_iota(jnp.int32, (BLOCK, 2 * BLOCK), 1)
    dist = BLOCK + r - c
    mask = (dist >= 0) & (dist <= BLOCK) & (c >= low)
    lane = lax.broadcasted_iota(jnp.int32, (BLOCK, LANES), 1)
    lo_half = lane < HEAD_DIM
    lane2 = lax.broadcasted_iota(jnp.int32, (2 * BLOCK, LANES), 1)
    lo_half2 = lane2 < HEAD_DIM

    if shared_kv:
        k2 = jnp.concatenate([kp_ref[0], kc_ref[0]], axis=0)
        v2 = jnp.concatenate([vp_ref[0], vc_ref[0]], axis=0)
        k2r = pltpu.roll(k2, HEAD_DIM, 1)
        v2r = pltpu.roll(v2, HEAD_DIM, 1)
        kdup = [jnp.where(lo_half2, k2, k2r).astype(BF16), jnp.where(lo_half2, k2r, k2).astype(BF16)]
        vdup = [jnp.where(lo_half2, v2, v2r).astype(BF16), jnp.where(lo_half2, v2r, v2).astype(BF16)]

    lse_tile = jnp.zeros((BLOCK, LANES), F32)
    n_pairs = N_HEADS // 2
    for jp in range(n_pairs):
        cs = slice(jp * LANES, (jp + 1) * LANES)
        q2 = q_ref[0, :, cs]
        if shared_kv:
            kv_head = jp // (n_pairs // 2)
            kk, vv = kdup[kv_head], vdup[kv_head]
        else:
            kk = jnp.concatenate([kp_ref[0, :, cs], kc_ref[0, :, cs]], axis=0).astype(BF16)
            vv = jnp.concatenate([vp_ref[0, :, cs], vc_ref[0, :, cs]], axis=0).astype(BF16)
        halves = (jnp.where(lo_half, q2, 0.0).astype(BF16), jnp.where(lo_half, 0.0, q2).astype(BF16))
        res = []
        for side, qh in enumerate(halves):
            head = 2 * jp + side
            s = jnp.where(mask, _dot_nt(qh, kk), NEG_INF)
            m = jnp.max(s, axis=1, keepdims=True)
            if shared_kv:
                sink = sink_ref[head]
                m = jnp.maximum(m, sink)
            p = jnp.exp(s - m)
            l = jnp.sum(p, axis=1, keepdims=True)
            if shared_kv:
                l = l + jnp.exp(sink - m)
            res.append(_dot(p.astype(BF16), vv) / l)
            if not shared_kv:
                lse_tile = jnp.where(lane == head, m + jnp.log(l), lse_tile)
        o_ref[0, :, cs] = jnp.where(lo_half, res[0], res[1])
    if not shared_kv:
        lse_ref[0] = lse_tile


def _band_attn_dilated(qkv, batch, seq, gi):
    d = A_DILATIONS[gi]
    sub = seq // d
    nb = sub // BLOCK
    n_col = qkv.shape[1] // D_MODEL
    x = qkv.reshape(batch, sub, d * qkv.shape[1])
    cur = lambda part: (lambda b, r, i: (b, i, r * n_col + part * 3 + gi))
    prev = lambda part: (lambda b, r, i: (b, jnp.maximum(i - 1, 0), r * n_col + part * 3 + gi))
    blk = (1, BLOCK, D_MODEL)
    o, lse = pl.pallas_call(
        functools.partial(_band_attn_body, shared_kv=False),
        grid=(batch, d, nb),
        in_specs=[pl.BlockSpec(blk, cur(0)), pl.BlockSpec(blk, prev(1)), pl.BlockSpec(blk, cur(1)),
                  pl.BlockSpec(blk, prev(2)), pl.BlockSpec(blk, cur(2))],
        out_specs=[pl.BlockSpec(blk, lambda b, r, i: (b, i, r)),
                   pl.BlockSpec((1, BLOCK, LANES), lambda b, r, i: (b, i, r))],
        out_shape=[jax.ShapeDtypeStruct((batch, sub, d * D_MODEL), F32),
                   jax.ShapeDtypeStruct((batch, sub, d * LANES), F32)],
        compiler_params=_cparams("parallel", "parallel", "arbitrary"),
    )(x, x, x, x, x)
    return o.reshape(batch * seq, D_MODEL), lse.reshape(batch * seq, LANES)


def _band_attn_shared(q, kv, sinks, batch, seq):
    nb = seq // BLOCK
    q3 = q.reshape(batch, seq, D_MODEL)
    kv3 = kv.reshape(batch, seq, 2 * LANES)
    kblk = (1, BLOCK, LANES)
    cur = lambda part: (lambda b, r, i: (b, i, part))
    prev = lambda part: (lambda b, r, i: (b, jnp.maximum(i - 1, 0), part))
    o = pl.pallas_call(
        functools.partial(_band_attn_body, shared_kv=True),
        grid=(batch, 1, nb),
        in_specs=[pl.BlockSpec(memory_space=pltpu.SMEM),
                  pl.BlockSpec((1, BLOCK, D_MODEL), lambda b, r, i: (b, i, 0)),
                  pl.BlockSpec(kblk, prev(0)), pl.BlockSpec(kblk, cur(0)),
                  pl.BlockSpec(kblk, prev(1)), pl.BlockSpec(kblk, cur(1))],
        out_specs=pl.BlockSpec((1, BLOCK, D_MODEL), lambda b, r, i: (b, i, 0)),
        out_shape=jax.ShapeDtypeStruct((batch, seq, D_MODEL), F32),
        compiler_params=_cparams("parallel", "parallel", "arbitrary"),
    )(sinks, q3, kv3, kv3, kv3, kv3)
    return o.reshape(batch * seq, D_MODEL)


def _rows_attn(qj, kc, vc, kn, vn, cmask, nmask, hs, hst, sink_row=None):
    sc = _dot((kc * qj).astype(BF16), hs)
    sn = _dot((kn * qj).astype(BF16), hs)
    if cmask is not None:
        sc = jnp.where(cmask, sc, NEG_INF)
    sn = jnp.where(nmask, sn, NEG_INF)
    m = jnp.maximum(jnp.max(sc, axis=0, keepdims=True), jnp.max(sn, axis=0, keepdims=True))
    if sink_row is not None:
        m = jnp.maximum(m, sink_row)
    pc = jnp.exp(sc - m)
    pn = jnp.exp(sn - m)
    l = jnp.sum(pc, axis=0, keepdims=True) + jnp.sum(pn, axis=0, keepdims=True)
    if sink_row is not None:
        l = l + jnp.exp(sink_row - m)
    pcx = _dot((pc / l).astype(BF16), hst)
    pnx = _dot((pn / l).astype(BF16), hst)
    o = jnp.sum(pcx * vc, axis=0, keepdims=True) + jnp.sum(pnx * vn, axis=0, keepdims=True)
    return o, m + jnp.log(l)


def _sample_attn_a_body(qkv_ref, c1_ref, c2_ref, c3_ref, hs_ref, hst_ref, o_ref, *, n_new):
    hs = hs_ref[...]
    hst = hst_ref[...]
    qkv = qkv_ref[0]
    row_c = lax.broadcasted_iota(jnp.int32, (BLOCK, LANES), 0)
    row_n = lax.broadcasted_iota(jnp.int32, (n_new, LANES), 0)
    row_d = lax.broadcasted_iota(jnp.int32, (n_new, D_MODEL), 0)
    outs, lses = [], []
    for gi, c_ref in enumerate((c1_ref, c2_ref, c3_ref)):
        q = qkv[:, gi * D_MODEL:(gi + 1) * D_MODEL]
        kn = qkv[:, (3 + gi) * D_MODEL:(4 + gi) * D_MODEL]
        vn = qkv[:, (6 + gi) * D_MODEL:(7 + gi) * D_MODEL]
        o_acc = jnp.zeros((n_new, D_MODEL), F32)
        l_acc = jnp.zeros((n_new, LANES), F32)
        for j in range(n_new):
            if gi == 0:
                col, cmask, nmask = 0, row_c >= j, row_n <= j
            else:
                col, cmask, nmask = j * 2 * D_MODEL, None, row_n == j
            kc = c_ref[0, :, col:col + D_MODEL]
            vc = c_ref[0, :, col + D_MODEL:col + 2 * D_MODEL]
            o, lse = _rows_attn(q[j:j + 1], kc, vc, kn, vn, cmask, nmask, hs, hst)
            o_acc = jnp.where(row_d == j, o, o_acc)
            l_acc = jnp.where(row_n == j, lse, l_acc)
        outs.append(o_acc)
        lses.append(l_acc)
    m = jnp.maximum(jnp.maximum(lses[0], lses[1]), lses[2])
    es = [jnp.exp(l - m) for l in lses]
    z = es[0] + es[1] + es[2]
    mix = None
    for e, o in zip(es, outs):
        term = _dot_split(e / z, hst) * o
        mix = term if mix is None else mix + term
    o_ref[0] = mix


def _sample_attn_a(qkv, caches, hs, hst, n_batch, n_new):
    q3 = qkv.reshape(n_batch, n_new, qkv.shape[1])
    row = 2 * D_MODEL
    views = [caches[0].reshape(n_batch, BLOCK, row)]
    specs = [pl.BlockSpec((1, BLOCK, row), lambda n: (n, 0, 0))]
    for c, d in zip(caches[1:], A_DILATIONS[1:]):
        views.append(c.reshape(n_batch, BLOCK, d * row))
        specs.append(pl.BlockSpec((1, BLOCK, n_new * row), lambda n: (n, 0, 0)))
    return pl.pallas_call(
        functools.partial(_sample_attn_a_body, n_new=n_new),
        grid=(n_batch,),
        in_specs=[pl.BlockSpec((1, n_new, qkv.shape[1]), lambda n: (n, 0, 0))] + specs
                 + [pl.BlockSpec(hs.shape, lambda n: (0, 0)), pl.BlockSpec(hst.shape, lambda n: (0, 0))],
        out_specs=pl.BlockSpec((1, n_new, D_MODEL), lambda n: (n, 0, 0)),
        out_shape=jax.ShapeDtypeStruct((n_batch, n_new, D_MODEL), F32),
        compiler_params=_cparams("parallel"),
    )(q3, *views, hs, hst).reshape(n_batch * n_new, D_MODEL)


def _expand_kv_heads(x):
    lane = lax.broadcasted_iota(jnp.int32, x.shape, 1)
    lo_half = lane < HEAD_DIM
    xr = pltpu.roll(x, HEAD_DIM, 1)
    d0 = jnp.where(lo_half, x, xr)
    d1 = jnp.where(lo_half, xr, x)
    reps = D_MODEL // LANES // 2
    return jnp.concatenate([d0] * reps + [d1] * reps, axis=1)


def _sample_attn_b_body(q_ref, kvn_ref, cache_ref, sink_ref, hs_ref, hst_ref, o_ref, *, n_new):
    hs = hs_ref[...]
    hst = hst_ref[...]
    q = q_ref[0]
    kc = _expand_kv_heads(cache_ref[0, :, 0:LANES])
    vc = _expand_kv_heads(cache_ref[0, :, LANES:2 * LANES])
    kvn = kvn_ref[0]
    kn = _expand_kv_heads(kvn[:, 0:LANES])
    vn = _expand_kv_heads(kvn[:, LANES:2 * LANES])
    row_c = lax.broadcasted_iota(jnp.int32, (BLOCK, LANES), 0)
    row_n = lax.broadcasted_iota(jnp.int32, (n_new, LANES), 0)
    row_d = lax.broadcasted_iota(jnp.int32, (n_new, D_MODEL), 0)
    o_acc = jnp.zeros((n_new, D_MODEL), F32)
    for j in range(n_new):
        o, _ = _rows_attn(q[j:j + 1], kc, vc, kn, vn, row_c >= j, row_n <= j, hs, hst, sink_ref[...])
        o_acc = jnp.where(row_d == j, o, o_acc)
    o_ref[0] = o_acc


def _sample_attn_b(q, kvn, cache, sink_row, hs, hst, n_batch, n_new):
    return pl.pallas_call(
        functools.partial(_sample_attn_b_body, n_new=n_new),
        grid=(n_batch,),
        in_specs=[pl.BlockSpec((1, n_new, D_MODEL), lambda n: (n, 0, 0)),
                  pl.BlockSpec((1, n_new, 2 * LANES), lambda n: (n, 0, 0)),
                  pl.BlockSpec((1, BLOCK, 2 * LANES), lambda n: (n, 0, 0)),
                  pl.BlockSpec((1, LANES), lambda n: (0, 0)),
                  pl.BlockSpec(hs.shape, lambda n: (0, 0)), pl.BlockSpec(hst.shape, lambda n: (0, 0))],
        out_specs=pl.BlockSpec((1, n_new, D_MODEL), lambda n: (n, 0, 0)),
        out_shape=jax.ShapeDtypeStruct((n_batch, n_new, D_MODEL), F32),
        compiler_params=_cparams("parallel"),
    )(q.reshape(n_batch, n_new, D_MODEL), kvn.reshape(n_batch, n_new, 2 * LANES),
      cache.reshape(n_batch, BLOCK, 2 * LANES), sink_row, hs, hst).reshape(n_batch * n_new, D_MODEL)


def _bitonic_sort_pairs(n):
    pairs = []
    k = 2
    while k <= n:
        j = k // 2
        while j >= 1:
            for i in range(n):
                l = i ^ j
                if l > i:
                    pairs.append((i, l, (i & k) == 0))
            j //= 2
        k *= 2
    return pairs


def _bitonic_merge_pairs(n):
    pairs = []
    j = n // 2
    while j >= 1:
        for i in range(n):
            l = i ^ j
            if l > i:
                pairs.append((i, l, True))
        j //= 2
    return pairs


_SORT16 = _bitonic_sort_pairs(PEER_TOPK)
_MERGE16 = _bitonic_merge_pairs(PEER_TOPK)


def _apply_network(xs, pairs):
    xs = list(xs)
    for i, l, desc in pairs:
        hi = jnp.maximum(xs[i], xs[l])
        lo = jnp.minimum(xs[i], xs[l])
        xs[i], xs[l] = (hi, lo) if desc else (lo, hi)
    return xs


def _merge_top16(xs, ys):
    z = [jnp.maximum(xs[i], ys[PEER_TOPK - 1 - i]) for i in range(PEER_TOPK)]
    return _apply_network(z, _MERGE16)


def _top16_of_keys(st):
    xs = [st[v * SUBLANES:(v + 1) * SUBLANES, :] for v in range(PEER_KEYS // SUBLANES)]
    xs = _apply_network(xs, _SORT16)
    for shift in (4, 2, 1):
        xs = _merge_top16(xs, [pltpu.roll(x, shift, 0) for x in xs])
    return xs


def _peer_pre_body(x_ref, g_ref, wpq_ref, keys_ref, ht_ref, s1_ref, s2_ref, sc_ref):
    x = x_ref[...]
    ms = jnp.mean(x * x, axis=-1, keepdims=True)
    h = x * lax.rsqrt(ms + NORM_EPS) * g_ref[...]
    ht_ref[...] = h.T.astype(BF16)
    q = _dot(h.astype(BF16), wpq_ref[...]).astype(BF16)
    tok = x.shape[0]
    sub = lax.broadcasted_iota(jnp.int32, (SUBLANES, tok), 0)
    tops = [[jnp.zeros((SUBLANES, tok), F32)] * PEER_TOPK for _ in range(2)]
    for hd in range(PEER_HEADS):
        for p, s_ref in enumerate((s1_ref, s2_ref)):
            col = (hd * 2 + p) * PEER_KEY_DIM
            st = _dot_nt(keys_ref[hd, p], q[:, col:col + PEER_KEY_DIM])
            s_ref[hd] = st
            top = _top16_of_keys(st)
            tops[p] = [jnp.where(sub == hd, t, acc) for t, acc in zip(top, tops[p])]
    a, b = tops
    row0 = [a[0] + b[j] for j in range(PEER_TOPK)]
    col0 = [a[i] + b[0] for i in range(1, PEER_TOPK)] + [jnp.full((SUBLANES, tok), SORT_PAD, F32)]
    rest = [a[i] + b[j] for i in range(1, PEER_TOPK) for j in range(1, PEER_TOPK) if (i + 1) * (j + 1) <= PEER_TOPK]
    pad = [jnp.full((SUBLANES, tok), SORT_PAD, F32)] * (2 * PEER_TOPK - len(rest))
    rest = rest + pad
    best = _merge_top16(row0, col0)
    best = _merge_top16(best, _apply_network(rest[:PEER_TOPK], _SORT16))
    best = _merge_top16(best, _apply_network(rest[PEER_TOPK:], _SORT16))
    top_sum = best[0]
    z = jnp.exp(best[0] - top_sum)
    for t in best[1:]:
        z = z + jnp.exp(t - top_sum)
    sc_ref[0] = best[PEER_TOPK - 1]
    sc_ref[1] = a[0]
    sc_ref[2] = b[0]
    sc_ref[3] = 1.0 / z


def _peer_pre(x, g, wpq, keys, tm):
    t, d = x.shape
    kshape = keys.shape
    return pl.pallas_call(
        _peer_pre_body,
        grid=(t // tm,),
        in_specs=[pl.BlockSpec((tm, d), lambda i: (i, 0)),
                  pl.BlockSpec((1, d), lambda i: (0, 0)),
                  pl.BlockSpec(wpq.shape, lambda i: (0, 0)),
                  pl.BlockSpec(kshape, lambda i: (0, 0, 0, 0))],
        out_specs=[pl.BlockSpec((d, tm), lambda i: (0, i)),
                   pl.BlockSpec((PEER_HEADS, PEER_KEYS, tm), lambda i: (0, 0, i)),
                   pl.BlockSpec((PEER_HEADS, PEER_KEYS, tm), lambda i: (0, 0, i)),
                   pl.BlockSpec((4, PEER_HEADS, tm), lambda i: (0, 0, i))],
        out_shape=[jax.ShapeDtypeStruct((d, t), BF16),
                   jax.ShapeDtypeStruct((PEER_HEADS, PEER_KEYS, t), F32),
                   jax.ShapeDtypeStruct((PEER_HEADS, PEER_KEYS, t), F32),
                   jax.ShapeDtypeStruct((4, PEER_HEADS, t), F32)],
        compiler_params=_cparams("parallel"),
    )(x, g.reshape(1, d), wpq, keys)


def _gelu(a):
    return 0.5 * a * (1.0 + lax.erf(a * (1.0 / math.sqrt(2.0))))


def _peer_dense_body(x_ref, ht_ref, s1_ref, s2_ref, sc_ref, u_ref, vt_ref, o_ref,
                     e1_ref, e2_ref, at_ref, wt_ref, acc_ref):
    j = pl.program_id(1)
    eb, tm = at_ref.shape

    @pl.when(j == 0)
    def _():
        acc_ref[...] = jnp.zeros_like(acc_ref)
        for hd in range(PEER_HEADS):
            a1 = sc_ref[1, hd:hd + 1, :]
            b1 = sc_ref[2, hd:hd + 1, :]
            rz = sc_ref[3, hd:hd + 1, :]
            e1_ref[hd] = jnp.exp(s1_ref[hd] - a1) * rz
            e2_ref[hd] = jnp.exp(s2_ref[hd] - b1)

    at_ref[...] = _dot(u_ref[...], ht_ref[...])
    n_i1 = eb // PEER_KEYS
    assert n_i1 == SUBLANES
    i1_rows = pl.ds(pl.multiple_of(j * SUBLANES, SUBLANES), SUBLANES)
    for i in range(n_i1):
        for c in range(tm // LANES):
            cs = slice(c * LANES, (c + 1) * LANES)
            bshape = (SUBLANES, LANES)
            s1b = [jnp.broadcast_to(s1_ref[hd, i1_rows, cs][i:i + 1], bshape) for hd in range(PEER_HEADS)]
            e1b = [jnp.broadcast_to(e1_ref[hd, i1_rows, cs][i:i + 1], bshape) for hd in range(PEER_HEADS)]
            thr = [jnp.broadcast_to(sc_ref[0, hd:hd + 1, cs], bshape) for hd in range(PEER_HEADS)]

            def rows_body(r, carry):
                parts = []
                for half in range(BF16_ROWS // SUBLANES):
                    base = pl.multiple_of(r * BF16_ROWS + half * SUBLANES, SUBLANES)
                    gate = jnp.zeros(bshape, F32)
                    for hd in range(PEER_HEADS):
                        s = s1b[hd] + s2_ref[hd, pl.ds(base, SUBLANES), cs]
                        w = jnp.where(s >= thr[hd], e2_ref[hd, pl.ds(base, SUBLANES), cs], 0.0)
                        gate = gate + e1b[hd] * w
                    act = _gelu(at_ref[pl.ds(i * PEER_KEYS + base, SUBLANES), cs])
                    parts.append(gate * act)
                out_rows = pl.ds(pl.multiple_of(i * PEER_KEYS + r * BF16_ROWS, BF16_ROWS), BF16_ROWS)
                wt_ref[out_rows, cs] = jnp.concatenate(parts, axis=0).astype(BF16)
                return carry

            lax.fori_loop(0, PEER_KEYS // BF16_ROWS, rows_body, 0)

    acc_ref[...] += _dot(vt_ref[...], wt_ref[...])

    @pl.when(j == pl.num_programs(1) - 1)
    def _():
        o_ref[...] = x_ref[...] + acc_ref[...].T


def _peer_dense(x, ht, s1, s2, sc, u, vt, tm, eb):
    t, d = x.shape
    return pl.pallas_call(
        _peer_dense_body,
        grid=(t // tm, N_EXPERTS // eb),
        in_specs=[pl.BlockSpec((tm, d), lambda i, j: (i, 0)),
                  pl.BlockSpec((d, tm), lambda i, j: (0, i)),
                  pl.BlockSpec((PEER_HEADS, PEER_KEYS, tm), lambda i, j: (0, 0, i)),
                  pl.BlockSpec((PEER_HEADS, PEER_KEYS, tm), lambda i, j: (0, 0, i)),
                  pl.BlockSpec((4, PEER_HEADS, tm), lambda i, j: (0, 0, i)),
                  pl.BlockSpec((eb, d), lambda i, j: (j, 0)),
                  pl.BlockSpec((d, eb), lambda i, j: (0, j))],
        out_specs=pl.BlockSpec((tm, d), lambda i, j: (i, 0)),
        out_shape=jax.ShapeDtypeStruct((t, d), F32),
        scratch_shapes=[pltpu.VMEM((PEER_HEADS, PEER_KEYS, tm), F32),
                        pltpu.VMEM((PEER_HEADS, PEER_KEYS, tm), F32),
                        pltpu.VMEM((eb, tm), F32),
                        pltpu.VMEM((eb, tm), BF16),
                        pltpu.VMEM((d, tm), F32)],
        compiler_params=_cparams("parallel", "arbitrary"),
    )(x, ht, s1, s2, sc, u, vt)


def _peer_ffn_residual(x, g, wpq, keys, u, vt, tm_pre, tm, eb):
    ht, s1, s2, sc = _peer_pre(x, g, wpq, keys, tm_pre)
    return _peer_dense(x, ht, s1, s2, sc, u, vt, tm, eb)


def _rope_tables(positions, reps):
    inv_freq = ROPE_THETA ** (-jnp.arange(HALF_DIM, dtype=F32) / HALF_DIM)
    ang = positions.astype(F32)[:, None] * inv_freq[None, :]
    cos, sin = jnp.cos(ang), jnp.sin(ang)
    cosf = jnp.concatenate([cos, cos, cos, cos], axis=1)
    sins = jnp.concatenate([-sin, sin, -sin, sin], axis=1)
    return jnp.tile(cosf, (reps, 1)), jnp.tile(sins, (reps, 1))


def _pair_gain(g, scale=1.0):
    return (jnp.concatenate([g, g]) * scale).reshape(1, LANES).astype(F32)


def _trunk(x, rope_tabs, caches_a, cache_b, n_batch, n_tok, consts, params, tiles):
    (norm_attn, norm_ffn, w_qkv_a, qnorm_a, knorm_a, w_o_a, norm_kv_b, w_kv_b, knorm_b, w_q_b, qnorm_b,
     sinks_b, w_o_b, w_pq, peer_keys, peer_u, peer_vt) = params
    bd, hs, hst = consts
    cosf, sins = rope_tabs
    tm, tm_pre, tm_peer, eb = tiles
    prompt = caches_a is None
    q_scale = 1.0 / math.sqrt(HEAD_DIM)
    n_a = w_qkv_a.shape[0]
    depth = norm_attn.shape[0]
    a_new = [[] for _ in A_DILATIONS]
    kvb = None
    b_new = None
    for layer in range(depth):
        if layer < n_a:
            rope = (_pair_gain(qnorm_a[layer], q_scale), _pair_gain(knorm_a[layer]), cosf, sins, bd)
            qkv = _norm_matmul(x, norm_attn[layer], w_qkv_a[layer], tm, D_MODEL, rope, n_q_tiles=3, n_qk_tiles=6)
            qkv5 = qkv.reshape(n_batch, n_tok, 3, len(A_DILATIONS), N_HEADS, HEAD_DIM)
            for gi, window in enumerate(A_WINDOWS):
                keep = min(window, n_tok)
                a_new[gi].append(jnp.stack([qkv5[:, n_tok - keep:, 1, gi], qkv5[:, n_tok - keep:, 2, gi]], axis=2))
            if prompt:
                outs, lses = zip(*[_band_attn_dilated(qkv, n_batch, n_tok, gi) for gi in range(len(A_DILATIONS))])
                x = _merge_proj_residual(x, outs, lses, hst, w_o_a[layer], tm)
            else:
                mix = _sample_attn_a(qkv, [c[layer] for c in caches_a], hs, hst, n_batch, n_tok)
                x = _proj_residual(x, mix, w_o_a[layer], tm)
        else:
            jb = layer - n_a
            if jb == 0:
                rope = (_pair_gain(knorm_b), _pair_gain(knorm_b), cosf, sins, bd)
                kvb = _norm_matmul(x, norm_kv_b, w_kv_b, tm, LANES, rope, n_q_tiles=0, n_qk_tiles=1)
                kv5 = kvb.reshape(n_batch, n_tok, 2, 2, HEAD_DIM)
                keep = min(BLOCK, n_tok) if prompt else n_tok
                b_new = kv5[:, n_tok - keep:]
            rope = (_pair_gain(qnorm_b[jb], q_scale), _pair_gain(qnorm_b[jb]), cosf, sins, bd)
            q = _norm_matmul(x, norm_attn[layer], w_q_b[jb], tm, D_MODEL, rope, n_q_tiles=1, n_qk_tiles=1)
            if prompt:
                o = _band_attn_shared(q, kvb, sinks_b[jb], n_batch, n_tok)
            else:
                sink_row = jnp.pad(sinks_b[jb], (0, LANES - N_HEADS)).reshape(1, LANES)
                o = _sample_attn_b(q, kvb, cache_b, sink_row, hs, hst, n_batch, n_tok)
            x = _proj_residual(x, o, w_o_b[jb], tm)
        x = _peer_ffn_residual(x, norm_ffn[layer], w_pq[layer], peer_keys[layer], peer_u[layer], peer_vt[layer],
                               tm_pre, tm_peer, eb)
    a_new = [jnp.stack(rows, axis=0) for rows in a_new]
    return x, a_new, b_new


def kernel(x_prompt, x_sample, cache_a_w128, cache_a_w512, cache_a_w2048, cache_b_kv, norm_attn, norm_ffn, w_qkv_a, qnorm_a, knorm_a, w_o_a, norm_kv_b, w_kv_b, knorm_b, w_q_b, qnorm_b, sinks_b, w_o_b, w_pq, peer_keys, peer_u, peer_v):
    batch, seq, d = x_prompt.shape
    dec_batch, dec_seq, _ = x_sample.shape
    assert d == D_MODEL and seq % (A_DILATIONS[-1] * BLOCK) == 0
    assert cache_a_w128.shape[2] == A_WINDOWS[0] and cache_a_w2048.shape[2] == A_WINDOWS[2]

    lane = jnp.arange(LANES)
    bd = jnp.where((lane[:, None] // HEAD_DIM) == (lane[None, :] // HEAD_DIM), 1.0 / HEAD_DIM, 0.0).astype(BF16)
    col = jnp.arange(D_MODEL)
    hs = (col[:, None] // HEAD_DIM == lane[None, :]).astype(BF16)
    hst = (lane[:, None] == col[None, :] // HEAD_DIM).astype(BF16)

    params = (norm_attn, norm_ffn, w_qkv_a.astype(BF16), qnorm_a, knorm_a, w_o_a.astype(BF16), norm_kv_b,
              w_kv_b.astype(BF16), knorm_b, w_q_b.astype(BF16), qnorm_b, sinks_b, w_o_b.astype(BF16),
              w_pq.astype(BF16), peer_keys.astype(BF16), peer_u.astype(BF16),
              jnp.swapaxes(peer_v, 1, 2).astype(BF16))
    consts = (bd, hs, hst)

    t_prompt = batch * seq
    t_sample = dec_batch * dec_seq
    tm_p = 512
    rope_p = _rope_tables(jnp.arange(seq, dtype=jnp.int32), 1)
    rope_s = _rope_tables(PAST_LEN + jnp.arange(dec_seq, dtype=jnp.int32), dec_batch)

    y_p, a_p, b_p = _trunk(x_prompt.reshape(t_prompt, d), rope_p, None, None, batch, seq, consts, params,
                           (tm_p, 256, 512, 1024))
    y_s, a_s, b_s = _trunk(x_sample.reshape(t_sample, d), rope_s,
                           (cache_a_w128, cache_a_w512, cache_a_w2048), cache_b_kv, dec_batch, dec_seq,
                           consts, params, (t_sample, min(256, t_sample), t_sample, 1024))
    return (y_p.reshape(batch, seq, d), y_s.reshape(dec_batch, dec_seq, d),
            a_p[0], a_p[1], a_p[2], b_p, a_s[0], a_s[1], a_s[2], b_s)
```
